```python
import jax, jax.numpy as jnp
from jax import lax
import numpy as np

D_MODEL = 1024
BATCH = 8
SEQ = 4096
DEPTH = 2

N_A_LAYERS = DEPTH // 2
N_B_LAYERS = DEPTH - N_A_LAYERS
D_FF = 2816
D_RNN = 1344
N_LRU_BLOCKS = 16
LRU_BLOCK = D_RNN // N_LRU_BLOCKS
CONV_WIDTH = 4
LRU_C = 8.0
N_HEADS = 16
HEAD_DIM = 64
D_ATTN = N_HEADS * HEAD_DIM
Q_BLOCK = 128
EPS = 1e-6

kernel_name = "yoco_rglru_forgetting_attention_macaron"


def rms_norm(x, g):
    xf = x.astype(jnp.float32)
    y = xf * lax.rsqrt(jnp.mean(xf * xf, axis=-1, keepdims=True) + EPS)
    return (y * g.astype(jnp.float32)).astype(x.dtype)


def swiglu(x, w_gate, w_up, w_down):
    return (jax.nn.silu(x @ w_gate) * (x @ w_up)) @ w_down


def causal_depthwise_conv(x, w, b):
    S = x.shape[1]
    xp = jnp.pad(x, ((0, 0), (CONV_WIDTH - 1, 0), (0, 0)))
    y = b
    for k in range(CONV_WIDTH):
        y = y + xp[:, k:k + S] * w[k]
    return y


def rg_lru(x, w_a, b_a, w_x, b_x, lam):
    Bn, S, _ = x.shape
    xb = x.reshape(Bn, S, N_LRU_BLOCKS, LRU_BLOCK)
    r = jax.nn.sigmoid(jnp.einsum('bsnc,ncd->bsnd', xb, w_a).reshape(Bn, S, D_RNN) + b_a)
    i = jax.nn.sigmoid(jnp.einsum('bsnc,ncd->bsnd', xb, w_x).reshape(Bn, S, D_RNN) + b_x)
    log_a = -LRU_C * r.astype(jnp.float32) * jax.nn.softplus(-lam.astype(jnp.float32))
    a = jnp.exp(log_a)
    u = jnp.sqrt(-jnp.expm1(2.0 * log_a)) * (i * x).astype(jnp.float32)

    def combine(c1, c2):
        a1, b1 = c1
        a2, b2 = c2
        return a1 * a2, a2 * b1 + b2

    _, h = lax.associative_scan(combine, (a, u), axis=1)
    return h.astype(x.dtype)


def recurrent_block(x, w_in, conv_w, conv_b, w_a, b_a, w_x, b_x, lam, w_out):
    gx = x @ w_in
    gate, rec = gx[..., :D_RNN], gx[..., D_RNN:]
    rec = causal_depthwise_conv(rec, conv_w, conv_b)
    rec = rg_lru(rec, w_a, b_a, w_x, b_x, lam)
    return (jax.nn.gelu(gate) * rec) @ w_out


def shared_kv(h, g, w_kv, w_f, b_f):
    Bn, S, _ = h.shape
    hn = rms_norm(h, g)
    kv = (hn @ w_kv).reshape(Bn, S, 2, N_HEADS, HEAD_DIM)
    k = kv[:, :, 0].transpose(0, 2, 1, 3)
    v = kv[:, :, 1].transpose(0, 2, 1, 3)
    log_f = jax.nn.log_sigmoid((hn @ w_f + b_f).astype(jnp.float32))
    c = jnp.cumsum(log_f, axis=1).transpose(0, 2, 1)
    return k, v, c


def forgetting_attention(xq, w_q, w_o, k, v, c):
    Bn, S, _ = xq.shape
    q = (xq @ w_q).reshape(Bn, S, N_HEADS, HEAD_DIM).transpose(0, 2, 1, 3) * (HEAD_DIM ** -0.5)
    outs = []
    for blk in range(S // Q_BLOCK):
        q0 = blk * Q_BLOCK
        end = q0 + Q_BLOCK
        logits = jnp.einsum('bhqd,bhkd->bhqk', q[:, :, q0:end], k[:, :, :end]).astype(jnp.float32)
        logits = logits + c[:, :, q0:end, None] - c[:, :, None, :end]
        qpos = q0 + jnp.arange(Q_BLOCK)
        kpos = jnp.arange(end)
        logits = jnp.where(kpos[None, :] <= qpos[:, None], logits, -jnp.inf)
        p = jax.nn.softmax(logits, axis=-1).astype(v.dtype)
        outs.append(jnp.einsum('bhqk,bhkd->bhqd', p, v[:, :, :end]))
    o = jnp.concatenate(outs, axis=2).transpose(0, 2, 1, 3).reshape(Bn, S, D_ATTN)
    return o @ w_o


def setup_inputs(seed: int = 0) -> dict:
    key = jax.random.key(seed)
    ks = iter(jax.random.split(key, 40))
    f32 = jnp.float32

    def nrm(shape, fan_in):
        return jax.random.normal(next(ks), shape, f32) * (fan_in ** -0.5)

    def gain(shape):
        return 1.0 + 0.05 * jax.random.normal(next(ks), shape, f32)

    def small(shape):
        return 0.02 * jax.random.normal(next(ks), shape, f32)

    L, NA, NB = DEPTH, N_A_LAYERS, N_B_LAYERS
    x = jax.random.normal(next(ks), (BATCH, SEQ, D_MODEL), f32)
    u = jax.random.uniform(next(ks), (NA, D_RNN), f32, minval=0.9, maxval=0.999)
    a0 = u ** (1.0 / LRU_C)
    rg_lambda = jnp.log(a0) - jnp.log1p(-a0)
    return {
        "x": x,
        "ffn1_pre_g": gain((L, D_MODEL)),
        "ffn1_w_gate": nrm((L, D_MODEL, D_FF), D_MODEL),
        "ffn1_w_up": nrm((L, D_MODEL, D_FF), D_MODEL),
        "ffn1_w_down": nrm((L, D_FF, D_MODEL), D_FF),
        "ffn1_post_g": gain((L, D_MODEL)),
        "mix_pre_g": gain((L, D_MODEL)),
        "mix_post_g": gain((L, D_MODEL)),
        "ffn2_pre_g": gain((L, D_MODEL)),
        "ffn2_w_gate": nrm((L, D_MODEL, D_FF), D_MODEL),
        "ffn2_w_up": nrm((L, D_MODEL, D_FF), D_MODEL),
        "ffn2_w_down": nrm((L, D_FF, D_MODEL), D_FF),
        "ffn2_post_g": gain((L, D_MODEL)),
        "rg_w_in": nrm((NA, D_MODEL, 2 * D_RNN), D_MODEL),
        "rg_conv_w": nrm((NA, CONV_WIDTH, D_RNN), CONV_WIDTH),
        "rg_conv_b": small((NA, D_RNN)),
        "rg_w_a": nrm((NA, N_LRU_BLOCKS, LRU_BLOCK, LRU_BLOCK), LRU_BLOCK),
        "rg_b_a": small((NA, D_RNN)),
        "rg_w_x": nrm((NA, N_LRU_BLOCKS, LRU_BLOCK, LRU_BLOCK), LRU_BLOCK),
        "rg_b_x": small((NA, D_RNN)),
        "rg_lambda": rg_lambda,
        "rg_w_out": nrm((NA, D_RNN, D_MODEL), D_RNN),
        "kv_norm_g": gain((D_MODEL,)),
        "w_kv": nrm((D_MODEL, 2 * D_ATTN), D_MODEL),
        "w_fgate": nrm((D_MODEL, N_HEADS), D_MODEL),
        "b_fgate": jax.random.uniform(next(ks), (N_HEADS,), f32, minval=1.0, maxval=4.0),
        "attn_w_q": nrm((NB, D_MODEL, D_ATTN), D_MODEL),
        "attn_w_o": nrm((NB, D_ATTN, D_MODEL), D_ATTN),
    }


def reference(x, ffn1_pre_g, ffn1_w_gate, ffn1_w_up, ffn1_w_down, ffn1_post_g,
              mix_pre_g, mix_post_g,
              ffn2_pre_g, ffn2_w_gate, ffn2_w_up, ffn2_w_down, ffn2_post_g,
              rg_w_in, rg_conv_w, rg_conv_b, rg_w_a, rg_b_a, rg_w_x, rg_b_x, rg_lambda, rg_w_out,
              kv_norm_g, w_kv, w_fgate, b_fgate, attn_w_q, attn_w_o):
    h = x
    k = v = c = None
    for layer in range(DEPTH):
        if layer == N_A_LAYERS:
            k, v, c = shared_kv(h, kv_norm_g, w_kv, w_fgate, b_fgate)
        f = swiglu(rms_norm(h, ffn1_pre_g[layer]), ffn1_w_gate[layer], ffn1_w_up[layer], ffn1_w_down[layer])
        h = h + 0.5 * rms_norm(f, ffn1_post_g[layer])
        hn = rms_norm(h, mix_pre_g[layer])
        if layer < N_A_LAYERS:
            j = layer
            m = recurrent_block(hn, rg_w_in[j], rg_conv_w[j], rg_conv_b[j], rg_w_a[j], rg_b_a[j],
                                rg_w_x[j], rg_b_x[j], rg_lambda[j], rg_w_out[j])
        else:
            j = layer - N_A_LAYERS
            m = forgetting_attention(hn, attn_w_q[j], attn_w_o[j], k, v, c)
        h = h + rms_norm(m, mix_post_g[layer])
        f = swiglu(rms_norm(h, ffn2_pre_g[layer]), ffn2_w_gate[layer], ffn2_w_up[layer], ffn2_w_down[layer])
        h = h + 0.5 * rms_norm(f, ffn2_post_g[layer])
    return h
```

```python
import functools
import math

import jax
import jax.numpy as jnp
from jax import lax
from jax.experimental import pallas as pl
from jax.experimental.pallas import tpu as pltpu

EPS = 1e-6
LRU_C = 8.0
LANES = 128
VMEM_LIMIT_BYTES = 56 * 1024 * 1024

FFN_ROWS = 512
REC_STEPS = 64
KV_STEPS = 64
ATTN_Q = 256
ATTN_K = 256

_BF16 = jnp.bfloat16
_F32 = jnp.float32


def _round_up(n, m):
    return (n + m - 1) // m * m


def _rms_norm(x, g):
    ms = jnp.mean(x * x, axis=-1, keepdims=True)
    return x * lax.rsqrt(ms + EPS) * g


def _dot(a, b):
    return jnp.dot(a, b, preferred_element_type=_F32)


def _resident(shape):
    return pl.BlockSpec(shape, lambda *_: (0,) * len(shape), pipeline_mode=pl.Buffered(1))


def _params(*sem):
    return pltpu.CompilerParams(dimension_semantics=sem, vmem_limit_bytes=VMEM_LIMIT_BYTES)


def _ffn_kernel(h_ref, pre_g_ref, wg_ref, wu_ref, wd_ref, post_g_ref, o_ref):
    x = h_ref[...]
    xn = _rms_norm(x, pre_g_ref[...]).astype(_BF16)
    g = _dot(xn, wg_ref[...])
    u = _dot(xn, wu_ref[...])
    a = (g * jax.nn.sigmoid(g) * u).astype(_BF16)
    f = _dot(a, wd_ref[...])
    o_ref[...] = x + 0.5 * _rms_norm(f, post_g_ref[...])


def _ffn(h, pre_g, wg, wu, wd, post_g):
    n, d = h.shape
    f = wg.shape[1]
    row = pl.BlockSpec((FFN_ROWS, d), lambda i: (i, 0))
    return pl.pallas_call(
        _ffn_kernel,
        grid=(n // FFN_ROWS,),
        in_specs=[row, _resident((1, d)), _resident((d, f)), _resident((d, f)),
                  _resident((f, d)), _resident((1, d))],
        out_specs=row,
        out_shape=jax.ShapeDtypeStruct((n, d), _F32),
        compiler_params=_params("parallel"),
        name="ffn",
    )(h, pre_g, wg, wu, wd, post_g)


def _softplus(x):
    return jnp.maximum(x, 0.0) + jnp.log1p(jnp.exp(-jnp.abs(x)))


def _gelu_tanh(x):
    return x * (0.5 * (1.0 + jnp.tanh(math.sqrt(2.0 / math.pi) * (x + 0.044715 * (x * x * x)))))


def _rglru_kernel(h_ref, pre_g_ref, wgate_ref, wrec_ref, convw_ref, convb_ref, wa_ref, ba_ref,
                  wx_ref, bx_ref, lam_ref, wout_ref, post_g_ref, o_ref,
                  rec_s, a_s, u_s, state_s, *, batch, steps, conv_width):
    rows = batch * steps
    halo = (conv_width - 1) * batch

    @pl.when(pl.program_id(0) == 0)
    def _():
        rec_s[0:halo, :] = jnp.zeros((halo, rec_s.shape[1]), _F32)
        state_s[...] = jnp.zeros(state_s.shape, _F32)

    x = h_ref[...]
    xn = _rms_norm(x, pre_g_ref[...]).astype(_BF16)
    gate = _dot(xn, wgate_ref[...])
    rec_s[halo:halo + rows, :] = _dot(xn, wrec_ref[...])

    y = convb_ref[...]
    for k in range(conv_width):
        y = y + convw_ref[k:k + 1, :] * rec_s[k * batch:k * batch + rows, :]
    rec_s[0:halo, :] = rec_s[rows:rows + halo, :]

    yb = y.astype(_BF16)
    r = jax.nn.sigmoid(_dot(yb, wa_ref[...]) + ba_ref[...])
    ig = jax.nn.sigmoid(_dot(yb, wx_ref[...]) + bx_ref[...])
    log_a = (-LRU_C) * r * _softplus(-lam_ref[...])
    a = jnp.exp(log_a)
    a_s[...] = a
    u_s[...] = jnp.sqrt(-jnp.tanh(log_a) * (1.0 + a * a)) * (ig * y)

    hstate = state_s[...]
    for t in range(steps):
        sl = slice(t * batch, (t + 1) * batch)
        hstate = a_s[sl, :] * hstate + u_s[sl, :]
        u_s[sl, :] = hstate
    state_s[...] = hstate

    m = _dot((_gelu_tanh(gate) * u_s[...]).astype(_BF16), wout_ref[...])
    o_ref[...] = x + _rms_norm(m, post_g_ref[...])


def _rglru(h, batch, pre_g, wgate, wrec, convw, convb, wa, ba, wx, bx, lam, wout, post_g):
    n, d = h.shape
    c = wgate.shape[1]
    conv_width = convw.shape[0]
    rows = batch * REC_STEPS
    row = pl.BlockSpec((rows, d), lambda i: (i, 0))
    kern = functools.partial(_rglru_kernel, batch=batch, steps=REC_STEPS, conv_width=conv_width)
    return pl.pallas_call(
        kern,
        grid=(n // rows,),
        in_specs=[row, _resident((1, d)), _resident((d, c)), _resident((d, c)),
                  _resident((conv_width, c)), _resident((1, c)),
                  _resident((c, c)), _resident((1, c)), _resident((c, c)), _resident((1, c)),
                  _resident((1, c)), _resident((c, d)), _resident((1, d))],
        out_specs=row,
        out_shape=jax.ShapeDtypeStruct((n, d), _F32),
        scratch_shapes=[pltpu.VMEM((rows + (conv_width - 1) * batch, c), _F32),
                        pltpu.VMEM((rows, c), _F32),
                        pltpu.VMEM((rows, c), _F32),
                        pltpu.VMEM((batch, c), _F32)],
        compiler_params=_params("arbitrary"),
        name="rglru",
    )(h, pre_g, wgate, wrec, convw, convb, wa, ba, wx, bx, lam, wout, post_g)


def _kv_kernel(h_ref, g_ref, wk_ref, wv_ref, wf_ref, bf_ref, k_ref, v_ref, c_ref, state_s,
               *, batch, steps):
    @pl.when(pl.program_id(0) == 0)
    def _():
        state_s[...] = jnp.zeros(state_s.shape, _F32)

    hn = _rms_norm(h_ref[...], g_ref[...]).astype(_BF16)
    k_ref[...] = _dot(hn, wk_ref[...]).astype(_BF16)
    v_ref[...] = _dot(hn, wv_ref[...]).astype(_BF16)
    log_f = -_softplus(-(_dot(hn, wf_ref[...]) + bf_ref[...]))
    c = state_s[...]
    for t in range(steps):
        sl = slice(t * batch, (t + 1) * batch)
        c = c + log_f[sl, :]
        c_ref[sl, :] = c
    state_s[...] = c


def _shared_kv(h, batch, g, wk, wv, wf, bf):
    n, d = h.shape
    da = wk.shape[1]
    rows = batch * KV_STEPS
    row = lambda w: pl.BlockSpec((rows, w), lambda i: (i, 0))
    kern = functools.partial(_kv_kernel, batch=batch, steps=KV_STEPS)
    return pl.pallas_call(
        kern,
        grid=(n // rows,),
        in_specs=[row(d), _resident((1, d)), _resident((d, da)), _resident((d, da)),
                  _resident((d, LANES)), _resident((1, LANES))],
        out_specs=[row(da), row(da), row(LANES)],
        out_shape=[jax.ShapeDtypeStruct((n, da), _BF16), jax.ShapeDtypeStruct((n, da), _BF16),
                   jax.ShapeDtypeStruct((n, LANES), _F32)],
        scratch_shapes=[pltpu.VMEM((batch, LANES), _F32)],
        compiler_params=_params("arbitrary"),
        name="shared_kv",
    )(h, g, wk, wv, wf, bf)


def _attn_kernel(h_ref, pre_g_ref, wq_ref, k_ref, v_ref, ccol_ref, crow_ref, wo_ref, post_g_ref,
                 o_ref, o_s, *, head_dim):
    qi = pl.program_id(1)
    tq, d = h_ref.shape
    tk = ATTN_K
    n_pairs = wq_ref.shape[1] // LANES
    x = h_ref[...]
    xn = _rms_norm(x, pre_g_ref[...]).astype(_BF16)
    q = (_dot(xn, wq_ref[...]) * (head_dim ** -0.5)).astype(_BF16)

    lo = lax.broadcasted_iota(jnp.int32, (1, LANES), 1) < head_dim
    causal = (lax.broadcasted_iota(jnp.int32, (tq, tk), 1)
              <= lax.broadcasted_iota(jnp.int32, (tq, tk), 0))
    nt = (((1,), (1,)), ((), ()))

    for p in range(n_pairs):
        lanes = slice(p * LANES, (p + 1) * LANES)
        q2 = q[:, lanes]
        qs = (jnp.where(lo, q2, jnp.zeros_like(q2)), jnp.where(lo, jnp.zeros_like(q2), q2))
        cq = (ccol_ref[:, 2 * p:2 * p + 1], ccol_ref[:, 2 * p + 1:2 * p + 2])

        def step(j, carry, masked, p=p, lanes=lanes, qs=qs, cq=cq):
            ms, ls, acc = carry
            start = pl.multiple_of(j * tk, tk)
            ks = k_ref[pl.ds(start, tk), lanes]
            vs = v_ref[pl.ds(start, tk), lanes]
            new_ms, new_ls, alphas, pvs = [], [], [], []
            for e in range(2):
                s = lax.dot_general(qs[e], ks, nt, preferred_element_type=_F32)
                logit = s + cq[e] - crow_ref[2 * p + e, pl.ds(j, 1), :]
                if masked:
                    logit = jnp.where(causal, logit, -jnp.inf)
                m_new = jnp.maximum(ms[e], jnp.max(logit, axis=1, keepdims=True))
                alpha = jnp.exp(ms[e] - m_new)
                pr = jnp.exp(logit - m_new)
                new_ms.append(m_new)
                new_ls.append(alpha * ls[e] + jnp.sum(pr, axis=1, keepdims=True))
                alphas.append(alpha)
                pvs.append(_dot(pr.astype(_BF16), vs))
            acc = jnp.where(lo, alphas[0], alphas[1]) * acc + jnp.where(lo, pvs[0], pvs[1])
            return tuple(new_ms), tuple(new_ls), acc

        neg = jnp.full((tq, 1), -jnp.inf, _F32)
        zero = jnp.zeros((tq, 1), _F32)
        init = ((neg, neg), (zero, zero), jnp.zeros((tq, LANES), _F32))
        carry = lax.fori_loop(0, qi, functools.partial(step, masked=False), init)
        _, ls, acc = step(qi, carry, masked=True)
        o_s[:, lanes] = (acc / jnp.where(lo, ls[0], ls[1])).astype(_BF16)

    m = _dot(o_s[...], wo_ref[...])
    o_ref[...] = x + _rms_norm(m, post_g_ref[...])


def _attention(h2, batch, pre_g, wq, k2, v2, ccol, crow, wo, post_g, head_dim):
    s, bd = h2.shape
    d = bd // batch
    da = wq.shape[1]
    n_heads = da // head_dim
    n_kt = s // ATTN_K
    kern = functools.partial(_attn_kernel, head_dim=head_dim)
    return pl.pallas_call(
        kern,
        grid=(batch, s // ATTN_Q),
        in_specs=[pl.BlockSpec((ATTN_Q, d), lambda b, i: (i, b)),
                  _resident((1, d)), _resident((d, da)),
                  pl.BlockSpec((s, da), lambda b, i: (0, b)),
                  pl.BlockSpec((s, da), lambda b, i: (0, b)),
                  pl.BlockSpec((None, ATTN_Q, n_heads), lambda b, i: (b, i, 0)),
                  pl.BlockSpec((None, n_heads, n_kt, ATTN_K), lambda b, i: (b, 0, 0, 0)),
                  _resident((da, d)), _resident((1, d))],
        out_specs=pl.BlockSpec((ATTN_Q, d), lambda b, i: (i, b)),
        out_shape=jax.ShapeDtypeStruct((s, bd), _F32),
        scratch_shapes=[pltpu.VMEM((ATTN_Q, da), _BF16)],
        compiler_params=_params("arbitrary", "arbitrary"),
        name="fox_attention",
    )(h2, pre_g, wq, k2, v2, ccol, crow, wo, post_g)


def _block_diag(w):
    nb, lb, _ = w.shape
    eye = jnp.eye(nb, dtype=w.dtype)
    return (w[:, :, None, :] * eye[:, None, :, None]).reshape(nb * lb, nb * lb)


def _pad_to(a, shape):
    return jnp.pad(a, [(0, t - s) for s, t in zip(a.shape, shape)])


def kernel(x, ffn1_pre_g, ffn1_w_gate, ffn1_w_up, ffn1_w_down, ffn1_post_g, mix_pre_g, mix_post_g, ffn2_pre_g, ffn2_w_gate, ffn2_w_up, ffn2_w_down, ffn2_post_g, rg_w_in, rg_conv_w, rg_conv_b, rg_w_a, rg_b_a, rg_w_x, rg_b_x, rg_lambda, rg_w_out, kv_norm_g, w_kv, w_fgate, b_fgate, attn_w_q, attn_w_o):
    batch, seq, d = x.shape
    depth = ffn1_pre_g.shape[0]
    n_rec = rg_w_in.shape[0]
    d_rnn = rg_w_in.shape[2] // 2
    c = _round_up(d_rnn, LANES)
    n_heads = w_fgate.shape[1]
    d_attn = w_kv.shape[1] // 2
    head_dim = d_attn // n_heads
    assert batch % 8 == 0 and 2 * head_dim == LANES and ATTN_Q == ATTN_K
    assert seq % max(REC_STEPS, KV_STEPS, ATTN_Q) == 0 and (seq * batch) % FFN_ROWS == 0

    row = lambda v: v.reshape(1, -1)
    bf = lambda w: w.astype(_BF16)

    def ffn(h, pre_g, wg, wu, wd, post_g):
        return _ffn(h, row(pre_g), bf(wg), bf(wu), bf(wd), row(post_g))

    h = x.transpose(1, 0, 2).reshape(seq * batch, d)
    for layer in range(depth):
        if layer == n_rec:
            wf = _pad_to(w_fgate, (d, LANES))
            k, v, cum = _shared_kv(h, batch, row(kv_norm_g), bf(w_kv[:, :d_attn]), bf(w_kv[:, d_attn:]),
                                   bf(wf), _pad_to(row(b_fgate), (1, LANES)))
            cum = cum[:, :n_heads].reshape(seq, batch, n_heads)
            ccol = cum.transpose(1, 0, 2)
            crow = cum.transpose(1, 2, 0).reshape(batch, n_heads, seq // ATTN_K, ATTN_K)
        h = ffn(h, ffn1_pre_g[layer], ffn1_w_gate[layer], ffn1_w_up[layer], ffn1_w_down[layer],
                ffn1_post_g[layer])
        if layer < n_rec:
            j = layer
            h = _rglru(
                h, batch, row(mix_pre_g[layer]),
                bf(_pad_to(rg_w_in[j][:, :d_rnn], (d, c))), bf(_pad_to(rg_w_in[j][:, d_rnn:], (d, c))),
                _pad_to(rg_conv_w[j], (rg_conv_w.shape[1], c)), _pad_to(row(rg_conv_b[j]), (1, c)),
                bf(_pad_to(_block_diag(rg_w_a[j]), (c, c))), _pad_to(row(rg_b_a[j]), (1, c)),
                bf(_pad_to(_block_diag(rg_w_x[j]), (c, c))), _pad_to(row(rg_b_x[j]), (1, c)),
                _pad_to(row(rg_lambda[j]), (1, c)), bf(_pad_to(rg_w_out[j], (c, d))),
                row(mix_post_g[layer]))
        else:
            j = layer - n_rec
            h2 = _attention(h.reshape(seq, batch * d), batch, row(mix_pre_g[layer]), bf(attn_w_q[j]),
                            k.reshape(seq, batch * d_attn), v.reshape(seq, batch * d_attn),
                            ccol, crow, bf(attn_w_o[j]), row(mix_post_g[layer]), head_dim)
            h = h2.reshape(seq * batch, d)
        h = ffn(h, ffn2_pre_g[layer], ffn2_w_gate[layer], ffn2_w_up[layer], ffn2_w_down[layer],
                ffn2_post_g[layer])
    return h.reshape(seq, batch, d).transpose(1, 0, 2)
```

```python
import functools
import math

import jax
import jax.numpy as jnp
from jax import lax
from jax.experimental import pallas as pl
from jax.experimental.pallas import tpu as pltpu

EPS = 1e-6
LRU_C = 8.0
LOG2E = math.log2(math.e)
LANES = 128
VMEM_LIMIT_BYTES = 56 * 1024 * 1024

FFN_STEPS = 64
REC_STEPS = 64
KV_STEPS = 128
ATTN_KC = 256
ATTN_K = 512
ATTN_PAIRS_PER_BLOCK = 8

_BF16 = jnp.bfloat16
_F32 = jnp.float32


def _round_up(n, m):
    return (n + m - 1) // m * m


def _rms_norm(x, g):
    ms = jnp.mean(x * x, axis=-1, keepdims=True)
    return x * lax.rsqrt(ms + EPS) * g


def _dot(a, b):
    return jnp.dot(a, b, preferred_element_type=_F32)


def _resident(shape):
    return pl.BlockSpec(shape, lambda *_: (0,) * len(shape), pipeline_mode=pl.Buffered(1))


def _params(*sem):
    return pltpu.CompilerParams(dimension_semantics=sem, vmem_limit_bytes=VMEM_LIMIT_BYTES)


def _swap_major(x, n0, n1):
    d = x.shape[-1]
    return jnp.swapaxes(x.reshape(n0, n1, d), 0, 1).reshape(n0 * n1, d)


def _ffn_kernel(h_ref, pre_g_ref, wg_ref, wu_ref, wd_ref, post_g_ref, o_ref, *, swap):
    d = h_ref.shape[-1]
    x = h_ref[...].reshape(-1, d)
    xn = _rms_norm(x, pre_g_ref[...]).astype(_BF16)
    g = _dot(xn, wg_ref[...])
    u = _dot(xn, wu_ref[...])
    a = (g * jax.nn.sigmoid(g) * u).astype(_BF16)
    f = _dot(a, wd_ref[...])
    y = x + 0.5 * _rms_norm(f, post_g_ref[...])
    if swap is not None:
        y = _swap_major(y, *swap)
    o_ref[...] = y.reshape(o_ref.shape)


def _ffn(h, batch, seq, in_time_major, out_time_major, pre_g, wg, wu, wd, post_g):
    n, d = h.shape
    f = wg.shape[1]
    steps = FFN_STEPS
    tm_spec = pl.BlockSpec((steps * batch, d), lambda i: (i, 0))
    bm_spec = pl.BlockSpec((batch, steps, d), lambda i: (0, i, 0))
    if in_time_major == out_time_major:
        args, in_spec, out_spec, swap = h, tm_spec, tm_spec, None
        out_shape = (n, d)
    elif out_time_major:
        args, in_spec, out_spec, swap = h.reshape(batch, seq, d), bm_spec, tm_spec, (batch, steps)
        out_shape = (n, d)
    else:
        args, in_spec, out_spec, swap = h, tm_spec, bm_spec, (steps, batch)
        out_shape = (batch, seq, d)
    out = pl.pallas_call(
        functools.partial(_ffn_kernel, swap=swap),
        grid=(seq // steps,),
        in_specs=[in_spec, _resident((1, d)), _resident((d, f)), _resident((d, f)),
                  _resident((f, d)), _resident((1, d))],
        out_specs=out_spec,
        out_shape=jax.ShapeDtypeStruct(out_shape, _F32),
        compiler_params=_params("parallel"),
        name="ffn",
    )(args, pre_g, wg, wu, wd, post_g)
    return out.reshape(n, d)


def _softplus(x):
    return jnp.maximum(x, 0.0) + jnp.log1p(jnp.exp(-jnp.abs(x)))


def _gelu_tanh(x):
    return x * (0.5 * (1.0 + jnp.tanh(math.sqrt(2.0 / math.pi) * (x + 0.044715 * (x * x * x)))))


def _rglru_kernel(h_ref, pre_g_ref, wgate_ref, wrec_ref, convw_ref, convb_ref, wa_ref, ba_ref,
                  wx_ref, bx_ref, lam_ref, wout_ref, post_g_ref, o_ref,
                  rec_s, a_s, u_s, state_s, *, batch, steps, conv_width):
    rows = batch * steps
    halo = (conv_width - 1) * batch

    @pl.when(pl.program_id(0) == 0)
    def _():
        rec_s[0:halo, :] = jnp.zeros((halo, rec_s.shape[1]), _F32)
        state_s[...] = jnp.zeros(state_s.shape, _F32)

    x = h_ref[...]
    xn = _rms_norm(x, pre_g_ref[...]).astype(_BF16)
    gate = _dot(xn, wgate_ref[...])
    rec_s[halo:halo + rows, :] = _dot(xn, wrec_ref[...])

    y = convb_ref[...]
    for k in range(conv_width):
        y = y + convw_ref[k:k + 1, :] * rec_s[k * batch:k * batch + rows, :]
    rec_s[0:halo, :] = rec_s[rows:rows + halo, :]

    yb = y.astype(_BF16)
    r = jax.nn.sigmoid(_dot(yb, wa_ref[...]) + ba_ref[...])
    ig = jax.nn.sigmoid(_dot(yb, wx_ref[...]) + bx_ref[...])
    log_a = (-LRU_C) * r * _softplus(-lam_ref[...])
    a = jnp.exp(log_a)
    a_s[...] = a
    u_s[...] = jnp.sqrt(-jnp.tanh(log_a) * (1.0 + a * a)) * (ig * y)

    hstate = state_s[...]
    for t in range(steps):
        sl = slice(t * batch, (t + 1) * batch)
        hstate = a_s[sl, :] * hstate + u_s[sl, :]
        u_s[sl, :] = hstate
    state_s[...] = hstate

    m = _dot((_gelu_tanh(gate) * u_s[...]).astype(_BF16), wout_ref[...])
    o_ref[...] = x + _rms_norm(m, post_g_ref[...])


def _rglru(h, batch, pre_g, wgate, wrec, convw, convb, wa, ba, wx, bx, lam, wout, post_g):
    n, d = h.shape
    c = wgate.shape[1]
    conv_width = convw.shape[0]
    rows = batch * REC_STEPS
    row = pl.BlockSpec((rows, d), lambda i: (i, 0))
    kern = functools.partial(_rglru_kernel, batch=batch, steps=REC_STEPS, conv_width=conv_width)
    return pl.pallas_call(
        kern,
        grid=(n // rows,),
        in_specs=[row, _resident((1, d)), _resident((d, c)), _resident((d, c)),
                  _resident((conv_width, c)), _resident((1, c)),
                  _resident((c, c)), _resident((1, c)), _resident((c, c)), _resident((1, c)),
                  _resident((1, c)), _resident((c, d)), _resident((1, d))],
        out_specs=row,
        out_shape=jax.ShapeDtypeStruct((n, d), _F32),
        scratch_shapes=[pltpu.VMEM((rows + (conv_width - 1) * batch, c), _F32),
                        pltpu.VMEM((rows, c), _F32),
                        pltpu.VMEM((rows, c), _F32),
                        pltpu.VMEM((batch, c), _F32)],
        compiler_params=_params("arbitrary"),
        name="rglru",
    )(h, pre_g, wgate, wrec, convw, convb, wa, ba, wx, bx, lam, wout, post_g)


def _kv_kernel(h_ref, g_ref, wk_ref, wv_ref, wf_ref, bf_ref, k_ref, v_ref, ccol_ref, crow_ref,
               c_s, state_s, *, n_heads):
    batch, steps, d = h_ref.shape
    n_pairs = k_ref.shape[1]

    @pl.when(pl.program_id(0) == 0)
    def _():
        state_s[...] = jnp.zeros(state_s.shape, _F32)

    hn = _rms_norm(h_ref[...].reshape(batch * steps, d), g_ref[...]).astype(_BF16)
    kf = _dot(hn, wk_ref[...]).astype(_BF16)
    vf = _dot(hn, wv_ref[...]).astype(_BF16)
    for p in range(n_pairs):
        lanes = slice(p * LANES, (p + 1) * LANES)
        k_ref[:, p] = kf[:, lanes].reshape(batch, steps, LANES)
        v_ref[:, p] = vf[:, lanes].reshape(batch, steps, LANES)
    log_f = (-LOG2E) * _softplus(-(_dot(hn, wf_ref[...]) + bf_ref[...]))
    log_f = jnp.swapaxes(log_f.reshape(batch, steps, LANES), 0, 1)
    c = state_s[...]
    for t in range(steps):
        c = c + log_f[t]
        c_s[t] = c
    state_s[...] = c
    cum = jnp.swapaxes(c_s[...], 0, 1)
    ccol_ref[...] = cum
    for b in range(batch):
        crow_ref[b] = cum[b].T[:n_heads, :]


def _shared_kv(h, g, wk, wv, wf, bf, n_heads):
    batch, seq, d = h.shape
    da = wk.shape[1]
    n_pairs = da // LANES
    steps = KV_STEPS
    blk = lambda w: pl.BlockSpec((batch, steps, w), lambda i: (0, i, 0))
    pair_blk = pl.BlockSpec((batch, n_pairs, steps, LANES), lambda i: (0, 0, i, 0))
    pair_shape = jax.ShapeDtypeStruct((batch, n_pairs, seq, LANES), _BF16)
    return pl.pallas_call(
        functools.partial(_kv_kernel, n_heads=n_heads),
        grid=(seq // steps,),
        in_specs=[blk(d), _resident((1, d)), _resident((d, da)), _resident((d, da)),
                  _resident((d, LANES)), _resident((1, LANES))],
        out_specs=[pair_blk, pair_blk, blk(LANES),
                   pl.BlockSpec((batch, n_heads, steps), lambda i: (0, 0, i))],
        out_shape=[pair_shape, pair_shape,
                   jax.ShapeDtypeStruct((batch, seq, LANES), _F32),
                   jax.ShapeDtypeStruct((batch, n_heads, seq), _F32)],
        scratch_shapes=[pltpu.VMEM((steps, batch, LANES), _F32), pltpu.VMEM((batch, LANES), _F32)],
        compiler_params=_params("arbitrary"),
        name="shared_kv",
    )(h, g, wk, wv, wf, bf)


def _attn_kernel(h_ref, pre_g_ref, wq_ref, k_ref, v_ref, ccol_ref, crow_ref, wo_ref, post_g_ref,
                 o_ref, q_s, o_s, cq_s, m_s, acc_s, *, head_dim):
    qi = pl.program_id(1)
    tq = h_ref.shape[0]
    n_pairs = q_s.shape[0]
    x = h_ref[...]
    xn = _rms_norm(x, pre_g_ref[...]).astype(_BF16)
    q = (_dot(xn, wq_ref[...]) * (head_dim ** -0.5 * LOG2E)).astype(_BF16)
    lo = lax.broadcasted_iota(jnp.int32, (1, LANES), 1) < head_dim
    for p in range(n_pairs):
        q2 = q[:, p * LANES:(p + 1) * LANES]
        q_s[p, :tq] = jnp.where(lo, q2, jnp.zeros_like(q2))
        q_s[p, tq:] = jnp.where(lo, jnp.zeros_like(q2), q2)
        cq_s[p, :tq] = jnp.broadcast_to(ccol_ref[:, 2 * p:2 * p + 1], (tq, LANES))
        cq_s[p, tq:] = jnp.broadcast_to(ccol_ref[:, 2 * p + 1:2 * p + 2], (tq, LANES))
    m_s[...] = jnp.full(m_s.shape, -jnp.inf, _F32)
    acc_s[...] = jnp.zeros(acc_s.shape, _F32)
    nt = (((1,), (1,)), ((), ()))

    def key_tile(chunk, n_chunks, masked):
        width = n_chunks * ATTN_KC
        start = pl.multiple_of(chunk * ATTN_KC, ATTN_KC)
        ones = jnp.ones((width, LANES), _BF16)
        if masked:
            row = lax.broadcasted_iota(jnp.int32, (2 * tq, width), 0)
            qpos = qi * tq + jnp.where(row >= tq, row - tq, row)
            visible = start + lax.broadcasted_iota(jnp.int32, (2 * tq, width), 1) <= qpos

        def group(g, carry):
            for i in range(ATTN_PAIRS_PER_BLOCK):
                p = g * ATTN_PAIRS_PER_BLOCK + i
                s = lax.dot_general(q_s[p], k_ref[p, pl.ds(start, width), :], nt,
                                    preferred_element_type=_F32)
                ck = [jnp.concatenate([crow_ref[2 * p + e, pl.ds(chunk + c, 1), :]
                                       for c in range(n_chunks)], axis=1) for e in range(2)]
                t = jnp.concatenate([s[:tq] - ck[0], s[tq:] - ck[1]], axis=0)
                if masked:
                    t = jnp.where(visible, t, -jnp.inf)
                cq = cq_s[p]
                m_old = m_s[p]
                m_new = jnp.maximum(m_old, jnp.max(t, axis=1, keepdims=True) + cq)
                m_s[p] = m_new
                shift = cq - m_new
                pr = jnp.exp2(t + jnp.concatenate([shift] * (width // LANES), axis=1)).astype(_BF16)
                vcat = jnp.concatenate([v_ref[p, pl.ds(start, width), :], ones], axis=1)
                alpha = jnp.exp2(m_old - m_new)
                acc_s[p] = jnp.concatenate([alpha, alpha], axis=1) * acc_s[p] + _dot(pr, vcat)
            return carry

        lax.fori_loop(0, n_pairs // ATTN_PAIRS_PER_BLOCK, group, 0)

    per_wide = ATTN_K // ATTN_KC
    n_wide = lax.div(qi, per_wide)

    def wide(j, carry):
        key_tile(j * per_wide, per_wide, False)
        return carry

    lax.fori_loop(0, n_wide, wide, 0)

    def leftover(c, carry):
        key_tile(c, 1, False)
        return carry

    lax.fori_loop(n_wide * per_wide, qi, leftover, 0)
    key_tile(qi, 1, True)

    for p in range(n_pairs):
        a = acc_s[p]
        o = jnp.where(lo, a[:tq, :LANES] / a[:tq, LANES:], a[tq:, :LANES] / a[tq:, LANES:])
        o_s[:, p * LANES:(p + 1) * LANES] = o.astype(_BF16)
    m = _dot(o_s[...], wo_ref[...])
    o_ref[...] = x + _rms_norm(m, post_g_ref[...])


def _attention(h, pre_g, wq, k, v, ccol, crow, wo, post_g, head_dim):
    batch, seq, d = h.shape
    da = wq.shape[1]
    n_heads = da // head_dim
    n_pairs = da // LANES
    tq = ATTN_KC
    per_batch = pl.BlockSpec((None, n_pairs, seq, LANES), lambda b, i: (b, 0, 0, 0),
                             pipeline_mode=pl.Buffered(1))
    q_tile = lambda w: pl.BlockSpec((None, tq, w), lambda b, i: (b, i, 0))
    return pl.pallas_call(
        functools.partial(_attn_kernel, head_dim=head_dim),
        grid=(batch, seq // tq),
        in_specs=[q_tile(d), _resident((1, d)), _resident((d, da)),
                  per_batch, per_batch, q_tile(LANES),
                  pl.BlockSpec((None, n_heads, seq // ATTN_KC, ATTN_KC), lambda b, i: (b, 0, 0, 0)),
                  _resident((da, d)), _resident((1, d))],
        out_specs=q_tile(d),
        out_shape=jax.ShapeDtypeStruct((batch, seq, d), _F32),
        scratch_shapes=[pltpu.VMEM((n_pairs, 2 * tq, LANES), _BF16),
                        pltpu.VMEM((tq, da), _BF16),
                        pltpu.VMEM((n_pairs, 2 * tq, LANES), _F32),
                        pltpu.VMEM((n_pairs, 2 * tq, LANES), _F32),
                        pltpu.VMEM((n_pairs, 2 * tq, 2 * LANES), _F32)],
        compiler_params=_params("arbitrary", "arbitrary"),
        name="fox_attention",
    )(h, pre_g, wq, k, v, ccol, crow, wo, post_g)


def _block_diag(w):
    nb, lb, _ = w.shape
    eye = jnp.eye(nb, dtype=w.dtype)
    return (w[:, :, None, :] * eye[:, None, :, None]).reshape(nb * lb, nb * lb)


def _pad_to(a, shape):
    return jnp.pad(a, [(0, t - s) for s, t in zip(a.shape, shape)])


def kernel(x, ffn1_pre_g, ffn1_w_gate, ffn1_w_up, ffn1_w_down, ffn1_post_g, mix_pre_g, mix_post_g, ffn2_pre_g, ffn2_w_gate, ffn2_w_up, ffn2_w_down, ffn2_post_g, rg_w_in, rg_conv_w, rg_conv_b, rg_w_a, rg_b_a, rg_w_x, rg_b_x, rg_lambda, rg_w_out, kv_norm_g, w_kv, w_fgate, b_fgate, attn_w_q, attn_w_o):
    batch, seq, d = x.shape
    depth = ffn1_pre_g.shape[0]
    n_rec = rg_w_in.shape[0]
    d_rnn = rg_w_in.shape[2] // 2
    c = _round_up(d_rnn, LANES)
    n_heads = w_fgate.shape[1]
    d_attn = w_kv.shape[1] // 2
    head_dim = d_attn // n_heads
    assert batch == 8 and 2 * head_dim == LANES and n_heads <= LANES
    assert ATTN_K % ATTN_KC == 0 and (d_attn // LANES) % ATTN_PAIRS_PER_BLOCK == 0
    assert seq % max(FFN_STEPS, REC_STEPS, KV_STEPS, ATTN_K) == 0

    row = lambda v: v.reshape(1, -1)
    bf = lambda w: w.astype(_BF16)

    def ffn(h, in_tm, out_tm, pre_g, wg, wu, wd, post_g):
        return _ffn(h, batch, seq, in_tm, out_tm, row(pre_g), bf(wg), bf(wu), bf(wd), row(post_g))

    h = x.reshape(batch * seq, d)
    for layer in range(depth):
        recurrent = layer < n_rec
        if layer == n_rec:
            k, v, ccol, crow = _shared_kv(
                h.reshape(batch, seq, d), row(kv_norm_g), bf(w_kv[:, :d_attn]), bf(w_kv[:, d_attn:]),
                bf(_pad_to(w_fgate, (d, LANES))), _pad_to(row(b_fgate), (1, LANES)), n_heads)
            crow = crow.reshape(batch, n_heads, seq // ATTN_KC, ATTN_KC)
        h = ffn(h, False, recurrent, ffn1_pre_g[layer], ffn1_w_gate[layer], ffn1_w_up[layer],
                ffn1_w_down[layer], ffn1_post_g[layer])
        if recurrent:
            j = layer
            h = _rglru(
                h, batch, row(mix_pre_g[layer]),
                bf(_pad_to(rg_w_in[j][:, :d_rnn], (d, c))), bf(_pad_to(rg_w_in[j][:, d_rnn:], (d, c))),
                _pad_to(rg_conv_w[j], (rg_conv_w.shape[1], c)), _pad_to(row(rg_conv_b[j]), (1, c)),
                bf(_pad_to(_block_diag(rg_w_a[j]), (c, c))), _pad_to(row(rg_b_a[j]), (1, c)),
                bf(_pad_to(_block_diag(rg_w_x[j]), (c, c))), _pad_to(row(rg_b_x[j]), (1, c)),
                _pad_to(row(rg_lambda[j]), (1, c)), bf(_pad_to(rg_w_out[j], (c, d))),
                row(mix_post_g[layer]))
        else:
            j = layer - n_rec
            h = _attention(h.reshape(batch, seq, d), row(mix_pre_g[layer]), bf(attn_w_q[j]), k, v,
                           ccol, crow, bf(attn_w_o[j]), row(mix_post_g[layer]), head_dim)
            h = h.reshape(batch * seq, d)
        h = ffn(h, recurrent, False, ffn2_pre_g[layer], ffn2_w_gate[layer], ffn2_w_up[layer],
                ffn2_w_down[layer], ffn2_post_g[layer])
    return h.reshape(batch, seq, d)
```

```python
import functools
import math

import jax
import jax.numpy as jnp
from jax import lax
from jax.experimental import pallas as pl
from jax.experimental.pallas import tpu as pltpu

EPS = 1e-6
LRU_C = 8.0
LOG2E = math.log2(math.e)
LANES = 128
VMEM_LIMIT_BYTES = 56 * 1024 * 1024

FFN_STEPS = 64
FFN_SUBTILES = 2
REC_STEPS = 64
GATE_BAND = 3 * LANES
KV_STEPS = 128
ATTN_KC = 256
ATTN_K = 512
ATTN_WIDE_UNROLL = 2
ATTN_PAIRS_PER_BLOCK = 8

_BF16 = jnp.bfloat16
_F32 = jnp.float32


def _round_up(n, m):
    return (n + m - 1) // m * m


def _rms_norm(x, g):
    ms = jnp.mean(x * x, axis=-1, keepdims=True)
    return x * lax.rsqrt(ms + EPS) * g


def _dot(a, b):
    return jnp.dot(a, b, preferred_element_type=_F32)


def _resident(shape):
    return pl.BlockSpec(shape, lambda *_: (0,) * len(shape), pipeline_mode=pl.Buffered(1))


def _params(*sem):
    return pltpu.CompilerParams(dimension_semantics=sem, vmem_limit_bytes=VMEM_LIMIT_BYTES)


def _swap_major(x, n0, n1):
    d = x.shape[-1]
    return jnp.swapaxes(x.reshape(n0, n1, d), 0, 1).reshape(n0 * n1, d)


def _ffn_kernel(h_ref, pre_g_ref, wg_ref, wu_ref, wd_ref, post_g_ref, o_ref, *, swap, n_sub):
    d = h_ref.shape[-1]
    time_axis_in = 1 if h_ref.ndim == 3 else 0
    time_axis_out = 1 if o_ref.ndim == 3 else 0
    sub_in = h_ref.shape[time_axis_in] // n_sub
    sub_out = o_ref.shape[time_axis_out] // n_sub
    for i in range(n_sub):
        idx_in = (slice(None),) * time_axis_in + (pl.ds(i * sub_in, sub_in),)
        idx_out = (slice(None),) * time_axis_out + (pl.ds(i * sub_out, sub_out),)
        x = h_ref[idx_in].reshape(-1, d)
        xn = _rms_norm(x, pre_g_ref[...]).astype(_BF16)
        g = _dot(xn, wg_ref[...])
        u = _dot(xn, wu_ref[...])
        a = (g * jax.nn.sigmoid(g) * u).astype(_BF16)
        f = _dot(a, wd_ref[...])
        y = x + 0.5 * _rms_norm(f, post_g_ref[...])
        if swap is not None:
            y = _swap_major(y, *swap)
        o_ref[idx_out] = y.reshape(o_ref[idx_out].shape)


def _ffn(h, batch, seq, in_time_major, out_time_major, pre_g, wg, wu, wd, post_g):
    n, d = h.shape
    f = wg.shape[1]
    steps = FFN_STEPS
    tile = steps * FFN_SUBTILES
    tm_spec = pl.BlockSpec((tile * batch, d), lambda i: (i, 0))
    bm_spec = pl.BlockSpec((batch, tile, d), lambda i: (0, i, 0))
    if in_time_major == out_time_major:
        args, in_spec, out_spec, swap = h, tm_spec, tm_spec, None
        out_shape = (n, d)
    elif out_time_major:
        args, in_spec, out_spec, swap = h.reshape(batch, seq, d), bm_spec, tm_spec, (batch, steps)
        out_shape = (n, d)
    else:
        args, in_spec, out_spec, swap = h, tm_spec, bm_spec, (steps, batch)
        out_shape = (batch, seq, d)
    out = pl.pallas_call(
        functools.partial(_ffn_kernel, swap=swap, n_sub=FFN_SUBTILES),
        grid=(seq // tile,),
        in_specs=[in_spec, _resident((1, d)), _resident((d, f)), _resident((d, f)),
                  _resident((f, d)), _resident((1, d))],
        out_specs=out_spec,
        out_shape=jax.ShapeDtypeStruct(out_shape, _F32),
        compiler_params=_params("parallel"),
        name="ffn",
    )(args, pre_g, wg, wu, wd, post_g)
    return out.reshape(n, d)


def _softplus(x):
    return jnp.maximum(x, 0.0) + jnp.log1p(jnp.exp(-jnp.abs(x)))


def _gelu_tanh(x):
    return x * (0.5 * (1.0 + jnp.tanh(math.sqrt(2.0 / math.pi) * (x + 0.044715 * (x * x * x)))))


def _band_start(j, c, band):
    return min(max((j - 1) * LANES, 0), c - band)


def _rglru_kernel(h_ref, pre_g_ref, win_ref, convw_ref, convb_ref, wband_ref, ba_ref, bx_ref,
                  lam_ref, wout_ref, post_g_ref, o_ref,
                  rec_s, a_s, u_s, state_s, *, batch, steps, conv_width):
    rows = batch * steps
    halo = (conv_width - 1) * batch
    c = a_s.shape[1]
    band = wband_ref.shape[1]

    @pl.when(pl.program_id(0) == 0)
    def _():
        rec_s[0:halo, :] = jnp.zeros((halo, c), _F32)
        state_s[...] = jnp.zeros(state_s.shape, _F32)

    x = h_ref[...]
    xn = _rms_norm(x, pre_g_ref[...]).astype(_BF16)
    gx = _dot(xn, win_ref[...])
    gate = gx[:, :c]
    rec_s[halo:halo + rows, :] = gx[:, c:]

    y = convb_ref[...]
    for k in range(conv_width):
        y = y + convw_ref[k:k + 1, :] * rec_s[k * batch:k * batch + rows, :]
    rec_s[0:halo, :] = rec_s[rows:rows + halo, :]

    yb = y.astype(_BF16)
    sp = _softplus(-lam_ref[...])
    for j in range(c // LANES):
        cs = slice(j * LANES, (j + 1) * LANES)
        lo = _band_start(j, c, band)
        g = _dot(yb[:, lo:lo + band], wband_ref[j])
        r = jax.nn.sigmoid(g[:, :LANES] + ba_ref[:, cs])
        ig = jax.nn.sigmoid(g[:, LANES:] + bx_ref[:, cs])
        log_a = (-LRU_C) * r * sp[:, cs]
        a = jnp.exp(log_a)
        var = -jnp.tanh(log_a) * (1.0 + a * a)
        scale = jnp.where(var > 0.0, var * lax.rsqrt(var), 0.0)
        a_s[:, cs] = a
        u_s[:, cs] = scale * (ig * y[:, cs])

    hstate = state_s[...]
    for t in range(steps):
        sl = slice(t * batch, (t + 1) * batch)
        hstate = a_s[sl, :] * hstate + u_s[sl, :]
        u_s[sl, :] = hstate
    state_s[...] = hstate

    m = _dot((_gelu_tanh(gate) * u_s[...]).astype(_BF16), wout_ref[...])
    o_ref[...] = x + _rms_norm(m, post_g_ref[...])


def _band_gate_weights(w_a, w_x, c, band):
    assert w_a.shape[1] <= LANES and band == 3 * LANES and c >= band
    wa = _pad_to(_block_diag(w_a), (c, c))
    wx = _pad_to(_block_diag(w_x), (c, c))
    chunks = []
    for j in range(c // LANES):
        lo = _band_start(j, c, band)
        cs = slice(j * LANES, (j + 1) * LANES)
        chunks.append(jnp.concatenate([wa[lo:lo + band, cs], wx[lo:lo + band, cs]], axis=1))
    return jnp.stack(chunks)


def _rglru(h, batch, pre_g, win, convw, convb, wband, ba, bx, lam, wout, post_g):
    n, d = h.shape
    c = wout.shape[0]
    conv_width = convw.shape[0]
    rows = batch * REC_STEPS
    row = pl.BlockSpec((rows, d), lambda i: (i, 0))
    kern = functools.partial(_rglru_kernel, batch=batch, steps=REC_STEPS, conv_width=conv_width)
    return pl.pallas_call(
        kern,
        grid=(n // rows,),
        in_specs=[row, _resident((1, d)), _resident((d, 2 * c)),
                  _resident((conv_width, c)), _resident((1, c)),
                  _resident(wband.shape), _resident((1, c)), _resident((1, c)),
                  _resident((1, c)), _resident((c, d)), _resident((1, d))],
        out_specs=row,
        out_shape=jax.ShapeDtypeStruct((n, d), _F32),
        scratch_shapes=[pltpu.VMEM((rows + (conv_width - 1) * batch, c), _F32),
                        pltpu.VMEM((rows, c), _F32),
                        pltpu.VMEM((rows, c), _F32),
                        pltpu.VMEM((batch, c), _F32)],
        compiler_params=_params("arbitrary"),
        name="rglru",
    )(h, pre_g, win, convw, convb, wband, ba, bx, lam, wout, post_g)


def _kv_kernel(h_ref, g_ref, wk_ref, wv_ref, wf_ref, bf_ref, k_ref, v_ref, ccol_ref, crow_ref,
               c_s, state_s, *, n_heads):
    batch, steps, d = h_ref.shape
    n_pairs = k_ref.shape[1]

    @pl.when(pl.program_id(0) == 0)
    def _():
        state_s[...] = jnp.zeros(state_s.shape, _F32)

    hn = _rms_norm(h_ref[...].reshape(batch * steps, d), g_ref[...]).astype(_BF16)
    kf = _dot(hn, wk_ref[...]).astype(_BF16)
    vf = _dot(hn, wv_ref[...]).astype(_BF16)
    for p in range(n_pairs):
        lanes = slice(p * LANES, (p + 1) * LANES)
        k_ref[:, p] = kf[:, lanes].reshape(batch, steps, LANES)
        v_ref[:, p] = vf[:, lanes].reshape(batch, steps, LANES)
    log_f = (-LOG2E) * _softplus(-(_dot(hn, wf_ref[...]) + bf_ref[...]))
    log_f = jnp.swapaxes(log_f.reshape(batch, steps, LANES), 0, 1)
    c = state_s[...]
    for t in range(steps):
        c = c + log_f[t]
        c_s[t] = c
    state_s[...] = c
    cum = jnp.swapaxes(c_s[...], 0, 1)
    ccol_ref[...] = cum
    for b in range(batch):
        crow_ref[b] = cum[b].T[:n_heads, :]


def _shared_kv(h, g, wk, wv, wf, bf, n_heads):
    batch, seq, d = h.shape
    da = wk.shape[1]
    n_pairs = da // LANES
    steps = KV_STEPS
    blk = lambda w: pl.BlockSpec((batch, steps, w), lambda i: (0, i, 0))
    pair_blk = pl.BlockSpec((batch, n_pairs, steps, LANES), lambda i: (0, 0, i, 0))
    pair_shape = jax.ShapeDtypeStruct((batch, n_pairs, seq, LANES), _BF16)
    return pl.pallas_call(
        functools.partial(_kv_kernel, n_heads=n_heads),
        grid=(seq // steps,),
        in_specs=[blk(d), _resident((1, d)), _resident((d, da)), _resident((d, da)),
                  _resident((d, LANES)), _resident((1, LANES))],
        out_specs=[pair_blk, pair_blk, blk(LANES),
                   pl.BlockSpec((batch, n_heads, steps), lambda i: (0, 0, i))],
        out_shape=[pair_shape, pair_shape,
                   jax.ShapeDtypeStruct((batch, seq, LANES), _F32),
                   jax.ShapeDtypeStruct((batch, n_heads, seq), _F32)],
        scratch_shapes=[pltpu.VMEM((steps, batch, LANES), _F32), pltpu.VMEM((batch, LANES), _F32)],
        compiler_params=_params("arbitrary"),
        name="shared_kv",
    )(h, g, wk, wv, wf, bf)


def _attn_kernel(h_ref, pre_g_ref, wq_ref, k_ref, v_ref, ccol_ref, crow_ref, wo_ref, post_g_ref,
                 o_ref, q_s, o_s, cq_s, m_s, acc_s, *, head_dim):
    qi = pl.program_id(1)
    tq = h_ref.shape[0]
    n_pairs = q_s.shape[0]
    x = h_ref[...]
    xn = _rms_norm(x, pre_g_ref[...]).astype(_BF16)
    q = (_dot(xn, wq_ref[...]) * (head_dim ** -0.5 * LOG2E)).astype(_BF16)
    lo = lax.broadcasted_iota(jnp.int32, (1, LANES), 1) < head_dim
    for p in range(n_pairs):
        q2 = q[:, p * LANES:(p + 1) * LANES]
        q_s[p, :tq] = jnp.where(lo, q2, jnp.zeros_like(q2))
        q_s[p, tq:] = jnp.where(lo, jnp.zeros_like(q2), q2)
        cq_s[p, :tq] = jnp.broadcast_to(ccol_ref[:, 2 * p:2 * p + 1], (tq, LANES))
        cq_s[p, tq:] = jnp.broadcast_to(ccol_ref[:, 2 * p + 1:2 * p + 2], (tq, LANES))
    nt = (((1,), (1,)), ((), ()))

    def key_tile(chunk, n_chunks, diagonal):
        width = n_chunks * ATTN_KC
        start = pl.multiple_of(chunk * ATTN_KC, ATTN_KC)
        ones = jnp.ones((width, LANES), _BF16)
        if diagonal:
            row = lax.broadcasted_iota(jnp.int32, (2 * tq, width), 0)
            qpos = qi * tq + jnp.where(row >= tq, row - tq, row)
            visible = start + lax.broadcasted_iota(jnp.int32, (2 * tq, width), 1) <= qpos

        def group(g, carry):
            for i in range(ATTN_PAIRS_PER_BLOCK):
                p = g * ATTN_PAIRS_PER_BLOCK + i
                s = lax.dot_general(q_s[p], k_ref[p, pl.ds(start, width), :], nt,
                                    preferred_element_type=_F32)
                ck = [jnp.concatenate([crow_ref[2 * p + e, pl.ds(chunk + c, 1), :]
                                       for c in range(n_chunks)], axis=1) for e in range(2)]
                t = jnp.concatenate([s[:tq] - ck[0], s[tq:] - ck[1]], axis=0)
                if diagonal:
                    t = jnp.where(visible, t, -jnp.inf)
                cq = cq_s[p]
                m_new = jnp.max(t, axis=1, keepdims=True) + cq
                if not diagonal:
                    m_old = m_s[p]
                    m_new = jnp.maximum(m_old, m_new)
                m_s[p] = m_new
                shift = cq - m_new
                pr = jnp.exp2(t + jnp.concatenate([shift] * (width // LANES), axis=1)).astype(_BF16)
                vcat = jnp.concatenate([v_ref[p, pl.ds(start, width), :], ones], axis=1)
                pv = _dot(pr, vcat)
                if diagonal:
                    acc_s[p] = pv
                else:
                    alpha = jnp.exp2(m_old - m_new)
                    acc_s[p] = jnp.concatenate([alpha, alpha], axis=1) * acc_s[p] + pv
            return carry

        lax.fori_loop(0, n_pairs // ATTN_PAIRS_PER_BLOCK, group, 0)

    key_tile(qi, 1, True)
    per_wide = ATTN_K // ATTN_KC
    n_wide = lax.div(qi, per_wide)

    n_multi = lax.div(n_wide, ATTN_WIDE_UNROLL)

    def multi(j, carry):
        for u in range(ATTN_WIDE_UNROLL):
            key_tile((j * ATTN_WIDE_UNROLL + u) * per_wide, per_wide, False)
        return carry

    lax.fori_loop(0, n_multi, multi, 0)

    def wide(j, carry):
        key_tile(j * per_wide, per_wide, False)
        return carry

    lax.fori_loop(n_multi * ATTN_WIDE_UNROLL, n_wide, wide, 0)

    def leftover(c, carry):
        key_tile(c, 1, False)
        return carry

    lax.fori_loop(n_wide * per_wide, qi, leftover, 0)

    for p in range(n_pairs):
        a = acc_s[p]
        o = jnp.where(lo, a[:tq, :LANES] / a[:tq, LANES:], a[tq:, :LANES] / a[tq:, LANES:])
        o_s[:, p * LANES:(p + 1) * LANES] = o.astype(_BF16)
    m = _dot(o_s[...], wo_ref[...])
    o_ref[...] = x + _rms_norm(m, post_g_ref[...])


def _attention(h, pre_g, wq, k, v, ccol, crow, wo, post_g, head_dim):
    batch, seq, d = h.shape
    da = wq.shape[1]
    n_heads = da // head_dim
    n_pairs = da // LANES
    tq = ATTN_KC
    per_batch = pl.BlockSpec((None, n_pairs, seq, LANES), lambda b, i: (b, 0, 0, 0),
                             pipeline_mode=pl.Buffered(1))
    q_tile = lambda w: pl.BlockSpec((None, tq, w), lambda b, i: (b, i, 0))
    return pl.pallas_call(
        functools.partial(_attn_kernel, head_dim=head_dim),
        grid=(batch, seq // tq),
        in_specs=[q_tile(d), _resident((1, d)), _resident((d, da)),
                  per_batch, per_batch, q_tile(LANES),
                  pl.BlockSpec((None, n_heads, seq // ATTN_KC, ATTN_KC), lambda b, i: (b, 0, 0, 0)),
                  _resident((da, d)), _resident((1, d))],
        out_specs=q_tile(d),
        out_shape=jax.ShapeDtypeStruct((batch, seq, d), _F32),
        scratch_shapes=[pltpu.VMEM((n_pairs, 2 * tq, LANES), _BF16),
                        pltpu.VMEM((tq, da), _BF16),
                        pltpu.VMEM((n_pairs, 2 * tq, LANES), _F32),
                        pltpu.VMEM((n_pairs, 2 * tq, LANES), _F32),
                        pltpu.VMEM((n_pairs, 2 * tq, 2 * LANES), _F32)],
        compiler_params=_params("arbitrary", "arbitrary"),
        name="fox_attention",
    )(h, pre_g, wq, k, v, ccol, crow, wo, post_g)


def _block_diag(w):
    nb, lb, _ = w.shape
    eye = jnp.eye(nb, dtype=w.dtype)
    return (w[:, :, None, :] * eye[:, None, :, None]).reshape(nb * lb, nb * lb)


def _pad_to(a, shape):
    return jnp.pad(a, [(0, t - s) for s, t in zip(a.shape, shape)])


def kernel(x, ffn1_pre_g, ffn1_w_gate, ffn1_w_up, ffn1_w_down, ffn1_post_g, mix_pre_g, mix_post_g, ffn2_pre_g, ffn2_w_gate, ffn2_w_up, ffn2_w_down, ffn2_post_g, rg_w_in, rg_conv_w, rg_conv_b, rg_w_a, rg_b_a, rg_w_x, rg_b_x, rg_lambda, rg_w_out, kv_norm_g, w_kv, w_fgate, b_fgate, attn_w_q, attn_w_o):
    batch, seq, d = x.shape
    depth = ffn1_pre_g.shape[0]
    n_rec = rg_w_in.shape[0]
    d_rnn = rg_w_in.shape[2] // 2
    c = _round_up(d_rnn, LANES)
    n_heads = w_fgate.shape[1]
    d_attn = w_kv.shape[1] // 2
    head_dim = d_attn // n_heads
    assert batch == 8 and 2 * head_dim == LANES and n_heads <= LANES
    assert ATTN_K % ATTN_KC == 0 and (d_attn // LANES) % ATTN_PAIRS_PER_BLOCK == 0
    assert seq % max(FFN_STEPS * FFN_SUBTILES, REC_STEPS, KV_STEPS, ATTN_K) == 0

    row = lambda v: v.reshape(1, -1)
    bf = lambda w: w.astype(_BF16)

    def ffn(h, in_tm, out_tm, pre_g, wg, wu, wd, post_g):
        return _ffn(h, batch, seq, in_tm, out_tm, row(pre_g), bf(wg), bf(wu), bf(wd), row(post_g))

    h = x.reshape(batch * seq, d)
    for layer in range(depth):
        recurrent = layer < n_rec
        if layer == n_rec:
            k, v, ccol, crow = _shared_kv(
                h.reshape(batch, seq, d), row(kv_norm_g), bf(w_kv[:, :d_attn]), bf(w_kv[:, d_attn:]),
                bf(_pad_to(w_fgate, (d, LANES))), _pad_to(row(b_fgate), (1, LANES)), n_heads)
            crow = crow.reshape(batch, n_heads, seq // ATTN_KC, ATTN_KC)
        h = ffn(h, False, recurrent, ffn1_pre_g[layer], ffn1_w_gate[layer], ffn1_w_up[layer],
                ffn1_w_down[layer], ffn1_post_g[layer])
        if recurrent:
            j = layer
            win = jnp.concatenate([_pad_to(rg_w_in[j][:, :d_rnn], (d, c)),
                                   _pad_to(rg_w_in[j][:, d_rnn:], (d, c))], axis=1)
            h = _rglru(
                h, batch, row(mix_pre_g[layer]), bf(win),
                _pad_to(rg_conv_w[j], (rg_conv_w.shape[1], c)), _pad_to(row(rg_conv_b[j]), (1, c)),
                bf(_band_gate_weights(rg_w_a[j], rg_w_x[j], c, GATE_BAND)),
                _pad_to(row(rg_b_a[j]), (1, c)), _pad_to(row(rg_b_x[j]), (1, c)),
                _pad_to(row(rg_lambda[j]), (1, c)), bf(_pad_to(rg_w_out[j], (c, d))),
                row(mix_post_g[layer]))
        else:
            j = layer - n_rec
            h = _attention(h.reshape(batch, seq, d), row(mix_pre_g[layer]), bf(attn_w_q[j]), k, v,
                           ccol, crow, bf(attn_w_o[j]), row(mix_post_g[layer]), head_dim)
            h = h.reshape(batch * seq, d)
        h = ffn(h, recurrent, False, ffn2_pre_g[layer], ffn2_w_gate[layer], ffn2_w_up[layer],
                ffn2_w_down[layer], ffn2_post_g[layer])
    return h.reshape(batch, seq, d)
```
